```python
import jax, jax.numpy as jnp
from jax import lax
import numpy as np

D_MODEL = 1024
BATCH = 4
SEQ = 8192
DEPTH = 4

N_META = 16
N_A = DEPTH // 2
N_B = DEPTH - N_A
N_DENSE = (DEPTH + 1) // 2
N_MOE = DEPTH // 2

M_HEADS = 8
M_QK = 64
M_V = D_MODEL // M_HEADS
M_CHUNK = 64
M_IN = 2 * M_HEADS * M_QK + M_HEADS * M_V + D_MODEL + 2 * M_HEADS

A_HEADS = 16
A_KV = 4
A_HD = 64
A_GROUP = A_HEADS // A_KV
WINDOW = 128
ROPE_THETA = 500000.0
ROPE_DIM = A_HD // 4

D_FF = 2816
N_EXP = 8
TOP_K = 2
D_FF_EXP = 3584
MOE_BLOCK = 256

ALPHA = (2.0 * DEPTH) ** 0.25
BETA = (8.0 * DEPTH) ** -0.25
LN_EPS = 1e-5
NEG = -1e30

kernel_name = "hybrid_mlstm_swa_yoco_moe"


def layer_norm(x, g, b):
    xf = x.astype(jnp.float32)
    mu = xf.mean(-1, keepdims=True)
    var = jnp.square(xf - mu).mean(-1, keepdims=True)
    return ((xf - mu) * lax.rsqrt(var + LN_EPS) * g.astype(jnp.float32) + b.astype(jnp.float32)).astype(x.dtype)


def rope_tables(length):
    half = ROPE_DIM // 2
    inv_freq = ROPE_THETA ** (-jnp.arange(half, dtype=jnp.float32) * 2.0 / ROPE_DIM)
    ang = jnp.arange(length, dtype=jnp.float32)[:, None] * inv_freq[None, :]
    return jnp.cos(ang), jnp.sin(ang)


def partial_rope(x, cos, sin):
    half = ROPE_DIM // 2
    xr = x[..., :ROPE_DIM].astype(jnp.float32)
    x1, x2 = xr[..., :half], xr[..., half:]
    c, s = cos[None, :, None, :], sin[None, :, None, :]
    rot = jnp.concatenate([x1 * c - x2 * s, x2 * c + x1 * s], -1).astype(x.dtype)
    return jnp.concatenate([rot, x[..., ROPE_DIM:]], -1)


def mlstm_chunkwise(q, k, v, i_pre, f_pre):
    bsz, length, n_h, dk = q.shape
    dv = v.shape[-1]
    f32 = jnp.float32
    pad = (-length) % M_CHUNK
    q = jnp.pad(q.astype(f32) * dk ** -0.5, ((0, 0), (pad, 0), (0, 0), (0, 0)))
    k = jnp.pad(k.astype(f32), ((0, 0), (pad, 0), (0, 0), (0, 0)))
    v = jnp.pad(v.astype(f32), ((0, 0), (pad, 0), (0, 0), (0, 0)))
    li = jnp.pad(i_pre.astype(f32), ((0, 0), (pad, 0), (0, 0)), constant_values=NEG)
    lf = jnp.pad(jax.nn.log_sigmoid(f_pre.astype(f32)), ((0, 0), (pad, 0), (0, 0)))
    n_c = (length + pad) // M_CHUNK

    def chunks(t):
        return t.reshape(bsz, n_c, M_CHUNK, n_h, -1).transpose(0, 3, 1, 2, 4)

    qc, kc, vc = chunks(q), chunks(k), chunks(v)
    lic = li.reshape(bsz, n_c, M_CHUNK, n_h).transpose(0, 3, 1, 2)
    lfc = lf.reshape(bsz, n_c, M_CHUNK, n_h).transpose(0, 3, 1, 2)
    b = jnp.cumsum(lfc, axis=-1)
    b_last = b[..., -1]

    a = b_last[..., None] - b + lic
    a_max = a.max(-1)
    w = jnp.exp(a - a_max[..., None])
    c_loc = jnp.einsum('bhnc,bhnck,bhncv->bhnkv', w, kc, vc)
    n_loc = jnp.einsum('bhnc,bhnck->bhnk', w, kc)

    def step(carry, inp):
        c_st, n_st, m_st = carry
        cl, nl, ml, g = inp
        m_new = jnp.maximum(g + m_st, ml)
        s_old = jnp.exp(g + m_st - m_new)
        s_new = jnp.exp(ml - m_new)
        c_new = s_old[..., None, None] * c_st + s_new[..., None, None] * cl
        n_new = s_old[..., None] * n_st + s_new[..., None] * nl
        return (c_new, n_new, m_new), (c_st, n_st, m_st)

    init = (jnp.zeros((bsz, n_h, dk, dv), f32), jnp.zeros((bsz, n_h, dk), f32), jnp.zeros((bsz, n_h), f32))
    xs = (jnp.moveaxis(c_loc, 2, 0), jnp.moveaxis(n_loc, 2, 0), jnp.moveaxis(a_max, 2, 0), jnp.moveaxis(b_last, 2, 0))
    _, (c0, n0, m0) = lax.scan(step, init, xs)
    c0, n0, m0 = jnp.moveaxis(c0, 0, 2), jnp.moveaxis(n0, 0, 2), jnp.moveaxis(m0, 0, 2)

    idx = jnp.arange(M_CHUNK)
    causal = idx[:, None] >= idx[None, :]
    dmat = jnp.where(causal, b[..., :, None] - b[..., None, :] + lic[..., None, :], NEG)
    inter = b + m0[..., None]
    m_t = jnp.maximum(dmat.max(-1), inter)
    s = jnp.einsum('bhnik,bhnjk->bhnij', qc, kc) * jnp.exp(dmat - m_t[..., None])
    s_inter = jnp.exp(inter - m_t)
    num = jnp.einsum('bhnij,bhnjv->bhniv', s, vc) + s_inter[..., None] * jnp.einsum('bhnik,bhnkv->bhniv', qc, c0)
    den = s.sum(-1) + s_inter * jnp.einsum('bhnik,bhnk->bhni', qc, n0)
    h = num / jnp.maximum(jnp.abs(den), jnp.exp(-m_t))[..., None]
    h = h.transpose(0, 2, 3, 1, 4).reshape(bsz, n_c * M_CHUNK, n_h, dv)
    return h[:, pad:]


def mlstm_mixer(x, w_in, b_gate, ln_h, w_out):
    bsz, length, _ = x.shape
    proj = x @ w_in
    o1 = M_HEADS * M_QK
    o2 = 2 * o1
    o3 = o2 + M_HEADS * M_V
    o4 = o3 + D_MODEL
    q = proj[..., :o1].reshape(bsz, length, M_HEADS, M_QK)
    k = proj[..., o1:o2].reshape(bsz, length, M_HEADS, M_QK)
    v = proj[..., o2:o3].reshape(bsz, length, M_HEADS, M_V)
    og = proj[..., o3:o4]
    gates = proj[..., o4:] + b_gate
    hm = mlstm_chunkwise(q, k, v, gates[..., :M_HEADS], gates[..., M_HEADS:])
    mu = hm.mean(-1, keepdims=True)
    var = jnp.square(hm - mu).mean(-1, keepdims=True)
    hn = (hm - mu) * lax.rsqrt(var + LN_EPS) * ln_h.astype(jnp.float32).reshape(M_HEADS, M_V)
    out = (hn.reshape(bsz, length, M_HEADS * M_V) * jax.nn.sigmoid(og.astype(jnp.float32))).astype(x.dtype)
    return out @ w_out


def shared_kv(h, w_kv, cos, sin):
    bsz, length, _ = h.shape
    kv = h @ w_kv
    k = kv[..., :A_KV * A_HD].reshape(bsz, length, A_KV, A_HD)
    v = kv[..., A_KV * A_HD:].reshape(bsz, length, A_KV, A_HD)
    return partial_rope(k, cos, sin), v


def swa_mixer(x, k, v, w_q, sinks, w_o, cos, sin):
    bsz, length, _ = x.shape
    n_real = length - N_META
    n_blk = n_real // WINDOW
    f32 = jnp.float32
    scale = A_HD ** -0.5
    q = partial_rope((x @ w_q).reshape(bsz, length, A_HEADS, A_HD), cos, sin)
    q = q.reshape(bsz, length, A_KV, A_GROUP, A_HD)
    sink = sinks.astype(f32).reshape(A_KV, A_GROUP)[:, :, None]
    qm, qr = q[:, :N_META], q[:, N_META:]
    km, kr = k[:, :N_META], k[:, N_META:]
    vm, vr = v[:, :N_META].astype(f32), v[:, N_META:]

    s_mm = jnp.einsum('bqhgd,bkhd->bhgqk', qm, km).astype(f32) * scale
    s_mm = jnp.where(jnp.tril(jnp.ones((N_META, N_META), bool)), s_mm, NEG)
    m_mm = jnp.maximum(s_mm.max(-1), sink)
    p_mm = jnp.exp(s_mm - m_mm[..., None])
    den_mm = p_mm.sum(-1) + jnp.exp(sink - m_mm)
    o_m = jnp.einsum('bhgqk,bkhd->bqhgd', p_mm / den_mm[..., None], vm)

    qb = qr.reshape(bsz, n_blk, WINDOW, A_KV, A_GROUP, A_HD)
    kb = kr.reshape(bsz, n_blk, WINDOW, A_KV, A_HD)
    vb = vr.reshape(bsz, n_blk, WINDOW, A_KV, A_HD)

    def with_prev(t):
        prev = jnp.pad(t, ((0, 0), (1, 0), (0, 0), (0, 0), (0, 0)))[:, :-1]
        return jnp.concatenate([prev, t], axis=2)

    k_band, v_band = with_prev(kb), with_prev(vb).astype(f32)
    s_band = jnp.einsum('bnqhgd,bnkhd->bnhgqk', qb, k_band).astype(f32) * scale
    s_meta = jnp.einsum('bnqhgd,bkhd->bnhgqk', qb, km).astype(f32) * scale
    qi = jnp.arange(WINDOW)[:, None]
    kj = jnp.arange(2 * WINDOW)[None, :]
    rel = qi + WINDOW - kj
    in_band = (rel >= 0) & (rel < WINDOW)
    blk_ok = (jnp.arange(n_blk)[:, None, None] > 0) | (kj >= WINDOW)[None]
    mask = in_band[None] & blk_ok
    s_band = jnp.where(mask[None, :, None, None], s_band, NEG)
    m = jnp.maximum(jnp.maximum(s_band.max(-1), s_meta.max(-1)), sink)
    p_band = jnp.exp(s_band - m[..., None])
    p_meta = jnp.exp(s_meta - m[..., None])
    inv_den = 1.0 / (p_band.sum(-1) + p_meta.sum(-1) + jnp.exp(sink - m))
    o_r = (jnp.einsum('bnhgqk,bnkhd->bnhgqd', p_band, v_band)
           + jnp.einsum('bnhgqk,bkhd->bnhgqd', p_meta, vm)) * inv_den[..., None]
    o_r = o_r.transpose(0, 1, 4, 2, 3, 5).reshape(bsz, n_real, A_HEADS * A_HD)
    o = jnp.concatenate([o_m.reshape(bsz, N_META, A_HEADS * A_HD), o_r], axis=1).astype(x.dtype)
    return o @ w_o


def swiglu(x, w_gu, w_down):
    gu = x @ w_gu
    return (jax.nn.silu(gu[..., :D_FF]) * gu[..., D_FF:]) @ w_down


def moe_swiglu(x, w_router, b_router, w_gu, w_down):
    bsz, length, d = x.shape
    n_tok = bsz * length
    f32 = jnp.float32
    xt = x.reshape(n_tok, d)
    logits = (xt @ w_router).astype(f32) + b_router.astype(f32)
    top_val, top_idx = lax.top_k(logits, TOP_K)
    gate = jax.nn.softmax(top_val, axis=-1)
    n_assign = n_tok * TOP_K
    flat_e = top_idx.reshape(-1)
    flat_tok = jnp.repeat(jnp.arange(n_tok, dtype=jnp.int32), TOP_K)
    flat_gate = gate.reshape(-1)
    order = jnp.argsort(flat_e)
    e_sorted = flat_e[order]
    counts = jnp.bincount(flat_e, length=N_EXP)
    padded = (counts + MOE_BLOCK - 1) // MOE_BLOCK * MOE_BLOCK
    start = jnp.cumsum(counts) - counts
    pad_end = jnp.cumsum(padded)
    pad_start = pad_end - padded
    dest = pad_start[e_sorted] + jnp.arange(n_assign) - start[e_sorted]
    n_rows = -(-n_assign // MOE_BLOCK) * MOE_BLOCK + N_EXP * MOE_BLOCK
    n_blocks = n_rows // MOE_BLOCK
    row_tok = jnp.zeros((n_rows,), jnp.int32).at[dest].set(flat_tok[order])
    row_gate = jnp.zeros((n_rows,), f32).at[dest].set(flat_gate[order])
    blk_exp = jnp.minimum(jnp.searchsorted(pad_end, jnp.arange(n_blocks) * MOE_BLOCK, side='right'), N_EXP - 1)
    xs = xt[row_tok].reshape(n_blocks, MOE_BLOCK, d)

    def expert_block(args):
        xb, e = args
        gu = xb @ w_gu[e]
        return (jax.nn.silu(gu[:, :D_FF_EXP]) * gu[:, D_FF_EXP:]) @ w_down[e]

    ys = lax.map(expert_block, (xs, blk_exp)).reshape(n_rows, d)
    out = jax.ops.segment_sum(ys * row_gate[:, None].astype(ys.dtype), row_tok, num_segments=n_tok)
    return out.reshape(bsz, length, d).astype(x.dtype)


def setup_inputs(seed: int = 0) -> dict:
    key = jax.random.key(seed)
    ks = jax.random.split(key, 24)
    f32 = jnp.float32

    def nrm(k, shape):
        return jax.random.normal(k, shape, f32)

    def dense(k, shape, fan_in, gain=1.0):
        return nrm(k, shape) * (gain * fan_in ** -0.5)

    return {
        "x": nrm(ks[0], (BATCH, SEQ, D_MODEL)),
        "meta": nrm(ks[1], (N_META, D_MODEL)),
        "w_in_a": dense(ks[2], (N_A, D_MODEL, M_IN), D_MODEL),
        "b_gate_a": jnp.concatenate([0.1 * nrm(ks[3], (N_A, M_HEADS)),
                                      3.0 + 0.5 * nrm(ks[4], (N_A, M_HEADS))], axis=-1),
        "ln_h_a": 1.0 + 0.02 * nrm(ks[5], (N_A, M_HEADS * M_V)),
        "w_out_a": dense(ks[6], (N_A, M_HEADS * M_V, D_MODEL), M_HEADS * M_V, BETA),
        "w_kv": dense(ks[7], (D_MODEL, 2 * A_KV * A_HD), D_MODEL),
        "w_q_b": dense(ks[8], (N_B, D_MODEL, A_HEADS * A_HD), D_MODEL),
        "sinks_b": 0.5 * nrm(ks[9], (N_B, A_HEADS)),
        "w_o_b": dense(ks[10], (N_B, A_HEADS * A_HD, D_MODEL), A_HEADS * A_HD, BETA),
        "w_gu_d": dense(ks[11], (N_DENSE, D_MODEL, 2 * D_FF), D_MODEL),
        "w_down_d": dense(ks[12], (N_DENSE, D_FF, D_MODEL), D_FF, BETA),
        "w_router": dense(ks[13], (N_MOE, D_MODEL, N_EXP), D_MODEL),
        "b_router": 0.01 * nrm(ks[14], (N_MOE, N_EXP)),
        "w_gu_e": dense(ks[15], (N_MOE, N_EXP, D_MODEL, 2 * D_FF_EXP), D_MODEL),
        "w_down_e": dense(ks[16], (N_MOE, N_EXP, D_FF_EXP, D_MODEL), D_FF_EXP, BETA),
        "ln_g": 1.0 + 0.02 * nrm(ks[17], (DEPTH, 2, D_MODEL)),
        "ln_b": 0.02 * nrm(ks[18], (DEPTH, 2, D_MODEL)),
    }


def reference(x, meta, w_in_a, b_gate_a, ln_h_a, w_out_a, w_kv, w_q_b, sinks_b, w_o_b,
              w_gu_d, w_down_d, w_router, b_router, w_gu_e, w_down_e, ln_g, ln_b):
    bsz = x.shape[0]
    h = jnp.concatenate([jnp.broadcast_to(meta.astype(x.dtype)[None], (bsz, N_META, D_MODEL)), x], axis=1)
    length = h.shape[1]
    cos, sin = rope_tables(length)
    k_sh, v_sh = None, None
    for layer in range(DEPTH):
        if layer < N_A:
            mix = mlstm_mixer(h, w_in_a[layer], b_gate_a[layer], ln_h_a[layer], w_out_a[layer])
        else:
            if layer == N_A:
                k_sh, v_sh = shared_kv(h, w_kv, cos, sin)
            j = layer - N_A
            mix = swa_mixer(h, k_sh, v_sh, w_q_b[j], sinks_b[j], w_o_b[j], cos, sin)
        h = layer_norm(ALPHA * h + mix, ln_g[layer, 0], ln_b[layer, 0])
        if layer % 2 == 0:
            ffn = swiglu(h, w_gu_d[layer // 2], w_down_d[layer // 2])
        else:
            e = layer // 2
            ffn = moe_swiglu(h, w_router[e], b_router[e], w_gu_e[e], w_down_e[e])
        h = layer_norm(ALPHA * h + ffn, ln_g[layer, 1], ln_b[layer, 1])
    return h[:, N_META:]
```

```python
import functools

import jax
import jax.numpy as jnp
from jax import lax
from jax.experimental import pallas as pl
from jax.experimental.pallas import tpu as pltpu

D_MODEL = 1024
DEPTH = 4
N_META = 16
N_A = DEPTH // 2
M_HEADS = 8
M_QK = 64
M_V = D_MODEL // M_HEADS
A_HEADS = 16
A_KV = 4
A_HD = 64
A_GROUP = A_HEADS // A_KV
WINDOW = 128
ROPE_THETA = 500000.0
ROPE_DIM = A_HD // 4
D_FF = 2816
N_EXP = 8
D_FF_EXP = 3584
ALPHA = (2.0 * DEPTH) ** 0.25
LN_EPS = 1e-5
NEG = -1e30

LANES = 128
CHUNK = 128
PAD = 128
DEAD = PAD - N_META
VMEM_LIMIT = 52 * 1024 * 1024
MOE_ROWS = 512
F32 = jnp.float32
BF16 = jnp.bfloat16
HIGHEST = lax.Precision.HIGHEST


def _cparams(sem):
    return pltpu.CompilerParams(dimension_semantics=sem, vmem_limit_bytes=VMEM_LIMIT)


def _pick_tile(n, cap, mult=CHUNK):
    best = None
    for t in range(mult, min(n, cap) + 1, mult):
        if n % t == 0:
            best = t
    assert best is not None, (n, cap, mult)
    return best


def _layer_norm(y, g, b):
    mu = jnp.mean(y, axis=-1, keepdims=True)
    yc = y - mu
    var = jnp.mean(yc * yc, axis=-1, keepdims=True)
    return yc * lax.rsqrt(var + LN_EPS) * g + b


def _sigmoid(x):
    return 1.0 / (1.0 + jnp.exp(-x))


def _mm_kernel(x_ref, w_ref, o_ref):
    o_ref[...] = jnp.dot(x_ref[...], w_ref[...], preferred_element_type=F32).astype(o_ref.dtype)


def _matmul(x, w, out_dtype, tm):
    m, k = x.shape
    n = w.shape[1]
    return pl.pallas_call(
        _mm_kernel,
        grid=(m // tm,),
        in_specs=[pl.BlockSpec((tm, k), lambda i: (i, 0)),
                  pl.BlockSpec((k, n), lambda i: (0, 0))],
        out_specs=pl.BlockSpec((tm, n), lambda i: (i, 0)),
        out_shape=jax.ShapeDtypeStruct((m, n), out_dtype),
        compiler_params=_cparams(("parallel",)),
        name="matmul",
    )(x, w)


def _mm_rope_kernel(x_ref, w_ref, c_ref, s_ref, o_ref):
    acc = jnp.dot(x_ref[...], w_ref[...], preferred_element_type=F32)
    c = c_ref[...]
    s = s_ref[...]
    lane = lax.broadcasted_iota(jnp.int32, c.shape, 1)
    first_half = (lane % ROPE_DIM) < (ROPE_DIM // 2)
    for ch in range(acc.shape[1] // LANES):
        a = acc[:, ch * LANES:(ch + 1) * LANES]
        partner = jnp.where(first_half,
                            pltpu.roll(a, LANES - ROPE_DIM // 2, 1),
                            pltpu.roll(a, ROPE_DIM // 2, 1))
        o_ref[:, ch * LANES:(ch + 1) * LANES] = (a * c + partner * s).astype(o_ref.dtype)


def _matmul_rope(x, w, cos_t, sin_t, tm):
    m, k = x.shape
    n = w.shape[1]
    nb = cos_t.shape[0]
    per = nb // tm
    return pl.pallas_call(
        _mm_rope_kernel,
        grid=(m // tm,),
        in_specs=[pl.BlockSpec((tm, k), lambda i: (i, 0)),
                  pl.BlockSpec((k, n), lambda i: (0, 0)),
                  pl.BlockSpec((tm, LANES), lambda i: (i % per, 0)),
                  pl.BlockSpec((tm, LANES), lambda i: (i % per, 0))],
        out_specs=pl.BlockSpec((tm, n), lambda i: (i, 0)),
        out_shape=jax.ShapeDtypeStruct((m, n), BF16),
        compiler_params=_cparams(("parallel",)),
        name="matmul_rope",
    )(x, w, cos_t, sin_t)


def _mm_res_ln_kernel(x_ref, w_ref, h_ref, g_ref, b_ref, of_ref, ob_ref):
    acc = jnp.dot(x_ref[...], w_ref[...], preferred_element_type=F32)
    out = _layer_norm(ALPHA * h_ref[...] + acc, g_ref[...], b_ref[...])
    of_ref[...] = out
    ob_ref[...] = out.astype(BF16)


def _matmul_res_ln(x, w, h, g, b, tm):
    m, k = x.shape
    n = w.shape[1]
    return pl.pallas_call(
        _mm_res_ln_kernel,
        grid=(m // tm,),
        in_specs=[pl.BlockSpec((tm, k), lambda i: (i, 0)),
                  pl.BlockSpec((k, n), lambda i: (0, 0)),
                  pl.BlockSpec((tm, n), lambda i: (i, 0)),
                  pl.BlockSpec((1, n), lambda i: (0, 0)),
                  pl.BlockSpec((1, n), lambda i: (0, 0))],
        out_specs=[pl.BlockSpec((tm, n), lambda i: (i, 0)),
                   pl.BlockSpec((tm, n), lambda i: (i, 0))],
        out_shape=[jax.ShapeDtypeStruct((m, n), F32), jax.ShapeDtypeStruct((m, n), BF16)],
        compiler_params=_cparams(("parallel",)),
        name="matmul_res_ln",
    )(x, w, h, g, b)


def _gate_act(z, is_f, dead):
    log_sig = jnp.minimum(z, 0.0) - jnp.log1p(jnp.exp(-jnp.abs(z)))
    live = jnp.where(is_f, log_sig, z)
    return jnp.where(dead, jnp.where(is_f, 0.0, NEG), live)


def _gates_kernel(x_ref, w_ref, wt_ref, bc_ref, br_ref, gc_ref, gr_ref, *, tm, nb):
    i = pl.program_id(0)
    x = x_ref[...]
    zc = jnp.dot(x, w_ref[...], preferred_element_type=F32) + bc_ref[...]
    zr = lax.dot_general(wt_ref[...], x, (((1,), (1,)), ((), ())),
                         preferred_element_type=F32) + br_ref[...]
    row_c = i * tm + lax.broadcasted_iota(jnp.int32, zc.shape, 0)
    lane_c = lax.broadcasted_iota(jnp.int32, zc.shape, 1)
    gc_ref[...] = _gate_act(zc, lane_c >= M_HEADS, (row_c % nb) < DEAD)
    row_r = i * tm + lax.broadcasted_iota(jnp.int32, zr.shape, 1)
    head_r = lax.broadcasted_iota(jnp.int32, zr.shape, 0)
    gr_ref[...] = _gate_act(zr, head_r >= M_HEADS, (row_r % nb) < DEAD)


def _gates(x, w_if, b_if, tm, nb):
    m, k = x.shape
    ng = 2 * M_HEADS
    w_c = jnp.zeros((k, LANES), BF16).at[:, :ng].set(w_if.astype(BF16))
    w_r = w_if.astype(BF16).T
    b_c = jnp.zeros((1, LANES), F32).at[0, :ng].set(b_if)
    b_r = b_if.reshape(ng, 1)
    return pl.pallas_call(
        functools.partial(_gates_kernel, tm=tm, nb=nb),
        grid=(m // tm,),
        in_specs=[pl.BlockSpec((tm, k), lambda i: (i, 0)),
                  pl.BlockSpec((k, LANES), lambda i: (0, 0)),
                  pl.BlockSpec((ng, k), lambda i: (0, 0)),
                  pl.BlockSpec((1, LANES), lambda i: (0, 0)),
                  pl.BlockSpec((ng, 1), lambda i: (0, 0))],
        out_specs=[pl.BlockSpec((tm, LANES), lambda i: (i, 0)),
                   pl.BlockSpec((ng, tm), lambda i: (0, i))],
        out_shape=[jax.ShapeDtypeStruct((m, LANES), F32), jax.ShapeDtypeStruct((ng, m), F32)],
        compiler_params=_cparams(("parallel",)),
        name="mlstm_gates",
    )(x, w_c, w_r, b_c, b_r)


def _mlstm_kernel(q_ref, k_ref, v_ref, og_ref, gc_ref, gr_ref, lnh_ref, tri_ref, trit_ref,
                  o_ref, caug_ref, m_ref, *, ts):
    @pl.when(pl.program_id(1) == 0)
    def _():
        caug_ref[...] = jnp.zeros_like(caug_ref)
        m_ref[...] = jnp.zeros_like(m_ref)

    tri = tri_ref[...]
    trit = trit_ref[...]
    ii = lax.broadcasted_iota(jnp.int32, (CHUNK, CHUNK), 0)
    jj = lax.broadcasted_iota(jnp.int32, (CHUNK, CHUNK), 1)
    causal = ii >= jj
    lane_v = lax.broadcasted_iota(jnp.int32, (CHUNK, M_V), 1)
    ones_col = jnp.where(lane_v == 0, 1.0, 0.0).astype(BF16)
    scale = M_QK ** -0.5

    for c in range(ts // CHUNK):
        rows = slice(c * CHUNK, (c + 1) * CHUNK)
        gcc = gc_ref[rows, :]
        grc = gr_ref[:, rows]
        cum_c = jnp.dot(tri, gcc, preferred_element_type=F32, precision=HIGHEST)
        cum_r = jnp.dot(grc, trit, preferred_element_type=F32, precision=HIGHEST)
        for h in range(M_HEADS):
            qh = q_ref[rows, h * M_QK:(h + 1) * M_QK]
            kh = k_ref[rows, h * M_QK:(h + 1) * M_QK]
            vh = v_ref[rows, h * M_V:(h + 1) * M_V]
            li_c = gcc[:, h:h + 1]
            li_r = grc[h:h + 1, :]
            b_c = cum_c[:, M_HEADS + h:M_HEADS + h + 1]
            b_r = cum_r[M_HEADS + h:M_HEADS + h + 1, :]
            b_last = b_r[:, CHUNK - 1:CHUNK]
            m0 = m_ref[h:h + 1, 0:1]
            caug = caug_ref[h]

            dmat = jnp.where(causal, b_c - b_r + li_r, NEG)
            inter = b_c + m0
            m_i = jnp.maximum(jnp.max(dmat, axis=1, keepdims=True), inter)
            s = lax.dot_general(qh, kh, (((1,), (1,)), ((), ())), preferred_element_type=F32) * scale
            p = s * jnp.exp(dmat - m_i)
            s_inter = jnp.exp(inter - m_i) * scale
            qc = jnp.dot(qh, caug.astype(BF16), preferred_element_type=F32)
            num = jnp.dot(p.astype(BF16), vh, preferred_element_type=F32) + s_inter * qc[:, :M_V]
            den = jnp.sum(p, axis=1, keepdims=True) + s_inter * qc[:, M_V:M_V + 1]
            hh = num / jnp.maximum(jnp.abs(den), jnp.exp(-m_i))

            a_c = b_last - b_c + li_c
            m_new = jnp.maximum(b_last + m0, jnp.max(a_c, axis=0, keepdims=True))
            kw = (kh.astype(F32) * jnp.exp(a_c - m_new)).astype(BF16)
            vaug = jnp.concatenate([vh, ones_col], axis=1)
            upd = lax.dot_general(kw, vaug, (((0,), (0,)), ((), ())), preferred_element_type=F32)
            caug_ref[h] = jnp.exp(b_last + m0 - m_new) * caug + upd
            m_ref[h:h + 1, :] = jnp.broadcast_to(m_new, (1, LANES))

            mu = jnp.mean(hh, axis=1, keepdims=True)
            hc = hh - mu
            var = jnp.mean(hc * hc, axis=1, keepdims=True)
            hn = hc * lax.rsqrt(var + LN_EPS) * lnh_ref[h:h + 1, :]
            gate = _sigmoid(og_ref[rows, h * M_V:(h + 1) * M_V])
            o_ref[rows, h * M_V:(h + 1) * M_V] = (hn * gate).astype(BF16)


def _mlstm(qkv, og, gc, gr, ln_h, bsz, nb):
    ts = _pick_tile(nb, 640)
    per = nb // ts
    ng = 2 * M_HEADS
    hq = M_HEADS * M_QK
    hv = M_HEADS * M_V
    idx = jnp.arange(CHUNK)
    tri = (idx[:, None] >= idx[None, :]).astype(F32)
    return pl.pallas_call(
        functools.partial(_mlstm_kernel, ts=ts),
        grid=(bsz, per),
        in_specs=[pl.BlockSpec((ts, hq), lambda b, s: (b * per + s, 0)),
                  pl.BlockSpec((ts, hq), lambda b, s: (b * per + s, 1)),
                  pl.BlockSpec((ts, hv), lambda b, s: (b * per + s, 2 * hq // hv)),
                  pl.BlockSpec((ts, hv), lambda b, s: (b * per + s, 0)),
                  pl.BlockSpec((ts, LANES), lambda b, s: (b * per + s, 0)),
                  pl.BlockSpec((ng, ts), lambda b, s: (0, b * per + s)),
                  pl.BlockSpec((M_HEADS, M_V), lambda b, s: (0, 0)),
                  pl.BlockSpec((CHUNK, CHUNK), lambda b, s: (0, 0)),
                  pl.BlockSpec((CHUNK, CHUNK), lambda b, s: (0, 0))],
        out_specs=pl.BlockSpec((ts, hv), lambda b, s: (b * per + s, 0)),
        out_shape=jax.ShapeDtypeStruct((bsz * nb, hv), BF16),
        scratch_shapes=[pltpu.VMEM((M_HEADS, M_QK, 2 * M_V), F32),
                        pltpu.VMEM((M_HEADS, LANES), F32)],
        compiler_params=_cparams(("parallel", "arbitrary")),
        name="mlstm_core",
    )(qkv, qkv, qkv, og, gc, gr, ln_h.reshape(M_HEADS, M_V), tri, tri.T)


def _swa_kernel(sink_ref, q_ref, ko_ref, kp_ref, km_ref, vo_ref, vp_ref, vm_ref, o_ref):
    n = pl.program_id(1)
    nq = A_GROUP * CHUNK
    nk = N_META + 2 * CHUNK
    qi = lax.broadcasted_iota(jnp.int32, (nq, nk), 0) % CHUNK
    col = lax.broadcasted_iota(jnp.int32, (nq, nk), 1)
    kp = col - N_META
    ko = col - N_META - CHUNK
    meta_lim = jnp.where(n >= 1, N_META, 0)
    prev_off = jnp.where(n >= 2, 0, CHUNK)
    own_lo = jnp.where(n >= 1, 0, DEAD)
    mask = ((col < meta_lim)
            | ((kp > qi + prev_off) & (kp < CHUNK))
            | ((ko >= own_lo) & (ko <= qi)))
    scale = A_HD ** -0.5
    qb = q_ref[...]
    pieces = []
    for g in range(A_KV):
        cols = slice(g * A_HD, (g + 1) * A_HD)
        kg = jnp.concatenate([km_ref[DEAD:CHUNK, cols], kp_ref[:, cols], ko_ref[:, cols]], axis=0)
        vg = jnp.concatenate([vm_ref[DEAD:CHUNK, cols], vp_ref[:, cols], vo_ref[:, cols]], axis=0)
        heads = [g * A_GROUP + j for j in range(A_GROUP)]
        qg = jnp.concatenate([qb[:, h * A_HD:(h + 1) * A_HD] for h in heads], axis=0)
        sink = jnp.concatenate([jnp.full((CHUNK, 1), sink_ref[h], F32) for h in heads], axis=0)
        s = lax.dot_general(qg, kg, (((1,), (1,)), ((), ())), preferred_element_type=F32) * scale
        s = jnp.where(mask, s, NEG)
        m = jnp.maximum(jnp.max(s, axis=1, keepdims=True), sink)
        p = jnp.exp(s - m)
        den = jnp.sum(p, axis=1, keepdims=True) + jnp.exp(sink - m)
        o = jnp.dot(p.astype(BF16), vg, preferred_element_type=F32) / den
        pieces += [o[j * CHUNK:(j + 1) * CHUNK] for j in range(A_GROUP)]
    o_ref[...] = jnp.concatenate(pieces, axis=1).astype(BF16)


def _swa(q, k, v, sinks, bsz, nb):
    per = nb // CHUNK
    dq = A_HEADS * A_HD
    dk = A_KV * A_HD
    own = lambda b, n: (b * per + n, 0)
    prev = lambda b, n: (b * per + jnp.maximum(n - 1, 0), 0)
    meta = lambda b, n: (b * per, 0)
    return pl.pallas_call(
        _swa_kernel,
        grid=(bsz, per),
        in_specs=[pl.BlockSpec(memory_space=pltpu.SMEM),
                  pl.BlockSpec((CHUNK, dq), own),
                  pl.BlockSpec((CHUNK, dk), own), pl.BlockSpec((CHUNK, dk), prev),
                  pl.BlockSpec((CHUNK, dk), meta),
                  pl.BlockSpec((CHUNK, dk), own), pl.BlockSpec((CHUNK, dk), prev),
                  pl.BlockSpec((CHUNK, dk), meta)],
        out_specs=pl.BlockSpec((CHUNK, dq), own),
        out_shape=jax.ShapeDtypeStruct((bsz * nb, dq), BF16),
        compiler_params=_cparams(("parallel", "parallel")),
        name="swa_attention",
    )(sinks, q, k, k, k, v, v, v)


def _ffn_body(nu_ref, x_ref, wg_ref, wu_ref, wd_ref, acc_ref):
    r = pl.program_id(0)
    j = pl.program_id(1)

    @pl.when(j == 0)
    def _():
        acc_ref[...] = jnp.zeros_like(acc_ref)

    @pl.when(r < nu_ref[0])
    def _():
        x = x_ref[...].astype(BF16)
        g = jnp.dot(x, wg_ref[...], preferred_element_type=F32)
        u = jnp.dot(x, wu_ref[...], preferred_element_type=F32)
        mid = (g * _sigmoid(g) * u).astype(BF16)
        acc_ref[...] += jnp.dot(mid, wd_ref[...], preferred_element_type=F32)


def _ffn_plain_kernel(be_ref, nu_ref, x_ref, wg_ref, wu_ref, wd_ref, o_ref, acc_ref):
    _ffn_body(nu_ref, x_ref, wg_ref, wu_ref, wd_ref, acc_ref)

    @pl.when(pl.program_id(1) == pl.num_programs(1) - 1)
    def _():
        o_ref[...] = acc_ref[...]


def _ffn_ln_kernel(be_ref, nu_ref, x_ref, wg_ref, wu_ref, wd_ref, h_ref, g_ref, b_ref,
                   of_ref, ob_ref, acc_ref):
    _ffn_body(nu_ref, x_ref, wg_ref, wu_ref, wd_ref, acc_ref)

    @pl.when(pl.program_id(1) == pl.num_programs(1) - 1)
    def _():
        out = _layer_norm(ALPHA * h_ref[...] + acc_ref[...], g_ref[...], b_ref[...])
        of_ref[...] = out
        ob_ref[...] = out.astype(BF16)


def _ffn(x, w_gu, w_down, blk_set, n_used, tm, tf, res=None):
    rows, d = x.shape
    ff = w_down.shape[1]
    nj = ff // tf
    x_spec = pl.BlockSpec((tm, d), lambda r, j, be, nu: (r, 0))
    in_specs = [x_spec,
                pl.BlockSpec((None, d, tf), lambda r, j, be, nu: (be[r], 0, j)),
                pl.BlockSpec((None, d, tf), lambda r, j, be, nu: (be[r], 0, nj + j)),
                pl.BlockSpec((None, tf, d), lambda r, j, be, nu: (be[r], j, 0))]
    args = [x, w_gu, w_gu, w_down]
    if res is None:
        kern = _ffn_plain_kernel
        out_specs = x_spec
        out_shape = jax.ShapeDtypeStruct((rows, d), F32)
    else:
        kern = _ffn_ln_kernel
        vec = pl.BlockSpec((1, d), lambda r, j, be, nu: (0, 0))
        in_specs += [x_spec, vec, vec]
        args += list(res)
        out_specs = [x_spec, x_spec]
        out_shape = [jax.ShapeDtypeStruct((rows, d), F32), jax.ShapeDtypeStruct((rows, d), BF16)]
    return pl.pallas_call(
        kern,
        grid_spec=pltpu.PrefetchScalarGridSpec(
            num_scalar_prefetch=2, grid=(rows // tm, nj),
            in_specs=in_specs, out_specs=out_specs,
            scratch_shapes=[pltpu.VMEM((tm, d), F32)]),
        out_shape=out_shape,
        compiler_params=_cparams(("parallel", "arbitrary")),
        name="swiglu_ffn",
    )(blk_set, n_used, *args)


def _router_kernel(h_ref, w_ref, b_ref, ltri_ref, ri_ref, rg_ref, cnt_ref, carry_ref):
    @pl.when(pl.program_id(0) == 0)
    def _():
        carry_ref[...] = jnp.zeros_like(carry_ref)

    logits = jnp.dot(h_ref[...], w_ref[...], preferred_element_type=F32, precision=HIGHEST) + b_ref[...]
    lane = lax.broadcasted_iota(jnp.int32, logits.shape, 1)
    lg = jnp.where(lane < N_EXP, logits, -jnp.inf)
    v1 = jnp.max(lg, axis=1, keepdims=True)
    e1 = jnp.min(jnp.where(lg == v1, lane, LANES), axis=1, keepdims=True)
    lg2 = jnp.where(lane == e1, -jnp.inf, lg)
    v2 = jnp.max(lg2, axis=1, keepdims=True)
    e2 = jnp.min(jnp.where(lg2 == v2, lane, LANES), axis=1, keepdims=True)
    t = jnp.exp(v2 - v1)
    g1 = 1.0 / (1.0 + t)
    g2 = t / (1.0 + t)
    oh1 = lane == e1
    oh2 = lane == e2
    oh = jnp.where(oh1 | oh2, 1.0, 0.0)
    rank = jnp.dot(ltri_ref[...], oh.astype(BF16), preferred_element_type=F32) + carry_ref[...]
    r1 = jnp.sum(jnp.where(oh1, rank, 0.0), axis=1, keepdims=True).astype(jnp.int32)
    r2 = jnp.sum(jnp.where(oh2, rank, 0.0), axis=1, keepdims=True).astype(jnp.int32)
    carry_ref[...] += jnp.sum(oh, axis=0, keepdims=True)
    ri_ref[...] = jnp.where(lane == 0, e1, jnp.where(lane == 1, e2,
                            jnp.where(lane == 2, r1, jnp.where(lane == 3, r2, 0))))
    rg_ref[...] = jnp.where(lane == 0, g1, jnp.where(lane == 1, g2, 0.0))
    cnt_ref[...] = carry_ref[...]


def _router(h, w_router, b_router, tm):
    m, d = h.shape
    w = jnp.zeros((d, LANES), F32).at[:, :N_EXP].set(w_router)
    b = jnp.zeros((1, LANES), F32).at[0, :N_EXP].set(b_router)
    idx = jnp.arange(tm)
    ltri = (idx[:, None] > idx[None, :]).astype(BF16)
    return pl.pallas_call(
        _router_kernel,
        grid=(m // tm,),
        in_specs=[pl.BlockSpec((tm, d), lambda i: (i, 0)),
                  pl.BlockSpec((d, LANES), lambda i: (0, 0)),
                  pl.BlockSpec((1, LANES), lambda i: (0, 0)),
                  pl.BlockSpec((tm, tm), lambda i: (0, 0))],
        out_specs=[pl.BlockSpec((tm, LANES), lambda i: (i, 0)),
                   pl.BlockSpec((tm, LANES), lambda i: (i, 0)),
                   pl.BlockSpec((1, LANES), lambda i: (0, 0))],
        out_shape=[jax.ShapeDtypeStruct((m, LANES), jnp.int32),
                   jax.ShapeDtypeStruct((m, LANES), F32),
                   jax.ShapeDtypeStruct((1, LANES), F32)],
        scratch_shapes=[pltpu.VMEM((1, LANES), F32)],
        compiler_params=_cparams(("arbitrary",)),
        name="moe_router",
    )(h, w, b, ltri)


def _row_copy(src_ref, src_row, dst_ref, dst_row, sem):
    return pltpu.make_async_copy(src_ref.at[pl.ds(src_row, 1)], dst_ref.at[pl.ds(dst_row, 1)], sem)


def _dispatch_kernel(d_ref, x_ref, init_ref, xs_ref, sem, *, tm):
    del init_ref

    def issue(t, carry):
        _row_copy(x_ref, t, xs_ref, d_ref[0, 0, t], sem).start()
        _row_copy(x_ref, t, xs_ref, d_ref[0, 0, tm + t], sem).start()
        return carry

    lax.fori_loop(0, tm, issue, 0)

    def drain(t, carry):
        _row_copy(x_ref, 0, xs_ref, 0, sem).wait()
        _row_copy(x_ref, 0, xs_ref, 0, sem).wait()
        return carry

    lax.fori_loop(0, tm, drain, 0)


def _dispatch(h, dest, n_rows, tm):
    m, d = h.shape
    return pl.pallas_call(
        functools.partial(_dispatch_kernel, tm=tm),
        grid=(m // tm,),
        in_specs=[pl.BlockSpec((1, 1, 2 * tm), lambda i: (i, 0, 0), memory_space=pltpu.SMEM),
                  pl.BlockSpec((tm, d), lambda i: (i, 0)),
                  pl.BlockSpec(memory_space=pl.ANY)],
        out_specs=pl.BlockSpec(memory_space=pl.ANY),
        out_shape=jax.ShapeDtypeStruct((n_rows, d), F32),
        scratch_shapes=[pltpu.SemaphoreType.DMA(())],
        input_output_aliases={2: 0},
        compiler_params=_cparams(("arbitrary",)),
        name="moe_dispatch",
    )(dest, h, jnp.zeros((n_rows, d), F32))


def _combine_kernel(d_ref, ys_ref, rg_ref, h_ref, g_ref, b_ref, of_ref, ob_ref,
                    buf1_ref, buf2_ref, sem, *, tm):
    def issue(t, carry):
        _row_copy(ys_ref, d_ref[0, 0, t], buf1_ref, t, sem).start()
        _row_copy(ys_ref, d_ref[0, 0, tm + t], buf2_ref, t, sem).start()
        return carry

    lax.fori_loop(0, tm, issue, 0)

    def drain(t, carry):
        _row_copy(ys_ref, 0, buf1_ref, 0, sem).wait()
        _row_copy(ys_ref, 0, buf2_ref, 0, sem).wait()
        return carry

    lax.fori_loop(0, tm, drain, 0)
    rg = rg_ref[...]
    ffn = rg[:, 0:1] * buf1_ref[...] + rg[:, 1:2] * buf2_ref[...]
    out = _layer_norm(ALPHA * h_ref[...] + ffn, g_ref[...], b_ref[...])
    of_ref[...] = out
    ob_ref[...] = out.astype(BF16)


def _combine(ys, dest, rg, h, g, b, tm):
    m, d = h.shape
    row = pl.BlockSpec((tm, d), lambda i: (i, 0))
    vec = pl.BlockSpec((1, d), lambda i: (0, 0))
    return pl.pallas_call(
        functools.partial(_combine_kernel, tm=tm),
        grid=(m // tm,),
        in_specs=[pl.BlockSpec((1, 1, 2 * tm), lambda i: (i, 0, 0), memory_space=pltpu.SMEM),
                  pl.BlockSpec(memory_space=pl.ANY),
                  pl.BlockSpec((tm, LANES), lambda i: (i, 0)),
                  row, vec, vec],
        out_specs=[row, row],
        out_shape=[jax.ShapeDtypeStruct((m, d), F32), jax.ShapeDtypeStruct((m, d), BF16)],
        scratch_shapes=[pltpu.VMEM((tm, d), F32), pltpu.VMEM((tm, d), F32),
                        pltpu.SemaphoreType.DMA(())],
        compiler_params=_cparams(("arbitrary",)),
        name="moe_combine",
    )(dest, ys, rg, h, g, b)


def _moe(h, w_router, b_router, w_gu, w_down, g, b, tm):
    m, d = h.shape
    ri, rg, cnt = _router(h, w_router, b_router, tm)
    counts = cnt[0, :N_EXP].astype(jnp.int32)
    padded = (counts + MOE_ROWS - 1) // MOE_ROWS * MOE_ROWS
    pad_end = jnp.cumsum(padded)
    pad_start = pad_end - padded
    n_blocks = 2 * m // MOE_ROWS + N_EXP
    n_rows = n_blocks * MOE_ROWS
    dest1 = pad_start[ri[:, 0]] + ri[:, 2]
    dest2 = pad_start[ri[:, 1]] + ri[:, 3]
    dest = jnp.concatenate([dest1.reshape(m // tm, 1, tm), dest2.reshape(m // tm, 1, tm)], axis=2)
    blk_start = jnp.arange(n_blocks, dtype=jnp.int32) * MOE_ROWS
    blk_exp = jnp.minimum(jnp.searchsorted(pad_end, blk_start, side="right"), N_EXP - 1).astype(jnp.int32)
    n_used = (pad_end[N_EXP - 1:] // MOE_ROWS).astype(jnp.int32)
    xs = _dispatch(h, dest, n_rows, tm)
    ys = _ffn(xs, w_gu, w_down, blk_exp, n_used, MOE_ROWS, _pick_tile(w_down.shape[1], 1024))
    return _combine(ys, dest, rg, h, g, b, tm)


def _rope_tables(nb):
    half = ROPE_DIM // 2
    inv_freq = ROPE_THETA ** (-jnp.arange(half, dtype=F32) * 2.0 / ROPE_DIM)
    pos = jnp.arange(nb, dtype=F32) - float(DEAD)
    ang = pos[:, None] * inv_freq[None, :]
    cos, sin = jnp.cos(ang), jnp.sin(ang)
    ones = jnp.ones((nb, A_HD - ROPE_DIM), F32)
    cos_h = jnp.concatenate([cos, cos, ones], axis=1)
    sin_h = jnp.concatenate([-sin, sin, 0.0 * ones], axis=1)
    reps = LANES // A_HD
    return jnp.tile(cos_h, (1, reps)), jnp.tile(sin_h, (1, reps))


def kernel(x, meta, w_in_a, b_gate_a, ln_h_a, w_out_a, w_kv, w_q_b, sinks_b, w_o_b,
           w_gu_d, w_down_d, w_router, b_router, w_gu_e, w_down_e, ln_g, ln_b):
    bsz, seq, d = x.shape
    assert d == D_MODEL and seq % CHUNK == 0
    nb = seq + PAD
    m = bsz * nb
    tm = _pick_tile(nb, 1024)
    front = jnp.concatenate([jnp.zeros((DEAD, d), x.dtype), meta.astype(x.dtype)], axis=0)
    h = jnp.concatenate([jnp.broadcast_to(front[None], (bsz, PAD, d)), x], axis=1).reshape(m, d)
    hb = h.astype(BF16)
    cos_t, sin_t = _rope_tables(nb)
    one_set = jnp.zeros((m // tm,), jnp.int32)
    all_used = jnp.full((1,), m // tm, jnp.int32)
    o_qk = 2 * M_HEADS * M_QK
    o_v = o_qk + M_HEADS * M_V
    o_og = o_v + D_MODEL
    k_sh = v_sh = None
    for layer in range(DEPTH):
        g0, b0 = ln_g[layer, 0].reshape(1, d), ln_b[layer, 0].reshape(1, d)
        g1, b1 = ln_g[layer, 1].reshape(1, d), ln_b[layer, 1].reshape(1, d)
        if layer < N_A:
            w_in = w_in_a[layer]
            qkv = _matmul(hb, w_in[:, :o_v].astype(BF16), BF16, tm)
            og = _matmul(hb, w_in[:, o_v:o_og].astype(BF16), F32, tm)
            gc, gr = _gates(hb, w_in[:, o_og:], b_gate_a[layer], tm, nb)
            mix_in = _mlstm(qkv, og, gc, gr, ln_h_a[layer], bsz, nb)
            w_mix = w_out_a[layer]
        else:
            if layer == N_A:
                k_sh = _matmul_rope(hb, w_kv[:, :A_KV * A_HD].astype(BF16), cos_t, sin_t, tm)
                v_sh = _matmul(hb, w_kv[:, A_KV * A_HD:].astype(BF16), BF16, tm)
            j = layer - N_A
            q = _matmul_rope(hb, w_q_b[j].astype(BF16), cos_t, sin_t, tm)
            mix_in = _swa(q, k_sh, v_sh, sinks_b[j], bsz, nb)
            w_mix = w_o_b[j]
        h, hb = _matmul_res_ln(mix_in, w_mix.astype(BF16), h, g0, b0, tm)
        if layer % 2 == 0:
            e = layer // 2
            h, hb = _ffn(hb, w_gu_d[e:e + 1].astype(BF16), w_down_d[e:e + 1].astype(BF16),
                         one_set, all_used, tm, _pick_tile(D_FF, 1536), res=(h, g1, b1))
        else:
            e = layer // 2
            h, hb = _moe(h, w_router[e], b_router[e], w_gu_e[e].astype(BF16),
                         w_down_e[e].astype(BF16), g1, b1, tm)
    return h.reshape(bsz, nb, d)[:, PAD:]
```
